```python
import math
import jax
import jax.numpy as jnp
from jax import lax
import numpy as np

D_MODEL = 1024
BATCH = 32
SEQ = 256
DEPTH = 4
DEC_BATCH = 2
DEC_SEQ = 1024
PAST_LEN = 256

GRID_W = 64
D_MIX = D_MODEL
HEAD_DIM = 64
ATT_W = D_MIX // 2
N_HEADS = ATT_W // HEAD_DIM
N_KV = 2
GQA_G = N_HEADS // N_KV
KV_W = N_KV * HEAD_DIM
Q_BLOCK = 128
ROPE_BASE = 10000.0
LRU_W = D_MIX // 4
LRU_BLOCKS = 4
LRU_BS = LRU_W // LRU_BLOCKS
LRU_C = 8.0
LRU_CONV = 4
HY_W = D_MIX // 4
HY_CONV = 3
HY_BANDS = 16
HY_FEAT = 2 * HY_BANDS + 1
HY_FILT_HID = 64
HY_DECAY_FAST = 0.3
HY_DECAY_SLOW = 1.5
HY_DECAY_TARGET = 1e-2
D_FF = -(-8 * D_MODEL // (3 * 256)) * 256
EPS = 1e-6
OFF_K = ATT_W
OFF_V = OFF_K + KV_W
OFF_LX = OFF_V + KV_W
OFF_LG = OFF_LX + LRU_W
OFF_HY = OFF_LG + LRU_W
D_IN = OFF_HY + 3 * HY_W

kernel_name = 'hybrid_dit_attn_rglru_hyena_step'


def rmsnorm(x, w):
    xf = x.astype(jnp.float32)
    y = xf * lax.rsqrt(jnp.mean(xf * xf, axis=-1, keepdims=True) + EPS)
    return (y * w.astype(jnp.float32)).astype(x.dtype)


def adaln_mod(cond, w_mod, b_mod):
    m = jax.nn.silu(cond) @ w_mod + b_mod
    return tuple(t[:, None, :] for t in jnp.split(m, 6, axis=-1))


def dwconv_centred(x, w, pad_left):
    K = w.shape[0]
    L = x.shape[1]
    xp = jnp.pad(x, ((0, 0), (pad_left, K - 1 - pad_left), (0, 0)))
    return sum(w[k] * xp[:, k:k + L] for k in range(K))


def axial_rope(x):
    L = x.shape[1]
    rows = L // GRID_W
    row = jnp.repeat(jnp.arange(rows, dtype=jnp.float32), GRID_W)
    col = jnp.tile(jnp.arange(GRID_W, dtype=jnp.float32), rows)
    nq = HEAD_DIM // 4
    freqs = ROPE_BASE ** (-jnp.arange(nq, dtype=jnp.float32) / nq)
    xf = x.astype(jnp.float32)

    def rot(xh, pos):
        ang = pos[:, None] * freqs[None, :]
        cos = jnp.cos(ang)[None, :, None, :]
        sin = jnp.sin(ang)[None, :, None, :]
        x1, x2 = xh[..., :nq], xh[..., nq:]
        return jnp.concatenate([x1 * cos - x2 * sin, x2 * cos + x1 * sin], axis=-1)

    half = HEAD_DIM // 2
    out = jnp.concatenate([rot(xf[..., :half], row), rot(xf[..., half:], col)], axis=-1)
    return out.astype(x.dtype)


def block_attention(q, k, v):
    B, Lq = q.shape[:2]
    nb = Lq // Q_BLOCK
    qb = q.reshape(B, nb, Q_BLOCK, N_KV, GQA_G, HEAD_DIM).transpose(1, 0, 2, 3, 4, 5)
    scale = HEAD_DIM ** -0.5

    def one_block(qblk):
        s = jnp.einsum('bqkgd,bskd->bkgqs', qblk, k).astype(jnp.float32) * scale
        p = jax.nn.softmax(s, axis=-1).astype(v.dtype)
        return jnp.einsum('bkgqs,bskd->bqkgd', p, v)

    o = lax.map(one_block, qb)
    return o.transpose(1, 0, 2, 3, 4, 5).reshape(B, Lq, ATT_W)


def rglru_coeffs(xc, gate_w, gate_b, lam):
    B, L = xc.shape[:2]
    xf = xc.astype(jnp.float32)
    xb = xf.reshape(B, L, LRU_BLOCKS, LRU_BS)
    g = jnp.einsum('blni,dgnij->dgblnj', xb, gate_w.astype(jnp.float32)).reshape(2, 2, B, L, LRU_W)
    g = g + gate_b.astype(jnp.float32)[:, :, None, None, :]
    r = jax.nn.sigmoid(g[:, 0])
    i = jax.nn.sigmoid(g[:, 1])
    log_a = -LRU_C * r * jax.nn.softplus(-lam.astype(jnp.float32))[:, None, None, :]
    a = jnp.exp(log_a)
    b = jnp.sqrt(-jnp.expm1(2.0 * log_a)) * (i * xf[None])
    return a, b


def _affine_combine(e1, e2):
    a1, b1 = e1
    a2, b2 = e2
    return a1 * a2, a2 * b1 + b2


def linear_recurrence(a, b, h0):
    A, Bc = lax.associative_scan(_affine_combine, (a, b), axis=1)
    return A * h0[:, None, :] + Bc


def bidir_scan(a, b, h0f, h0b):
    hf = linear_recurrence(a[0], b[0], h0f)
    hb = jnp.flip(linear_recurrence(jnp.flip(a[1], 1), jnp.flip(b[1], 1), h0b), 1)
    return hf, hb


def hyena_filter_spectra(L, w1, b1, w2, b2, w3, b3):
    f32 = jnp.float32
    t = jnp.arange(L, dtype=f32)
    tn = t / L
    bands = jnp.linspace(1e-4, HY_BANDS - 1, HY_BANDS, dtype=f32)
    ang = (2.0 * math.pi / L) * t[:, None] * bands[None, :]
    z = jnp.concatenate([tn[:, None], jnp.cos(ang), -jnp.sin(ang)], axis=-1)
    h = jnp.sin(z @ w1.astype(f32) + b1.astype(f32))
    h = jnp.sin(h @ w2.astype(f32) + b2.astype(f32))
    hf = (h @ w3.astype(f32) + b3.astype(f32)).reshape(L, 2, 2, HY_W)
    deltas = jnp.linspace(math.log(HY_DECAY_TARGET) / HY_DECAY_SLOW,
                          math.log(HY_DECAY_TARGET) / HY_DECAY_FAST, HY_W, dtype=f32)
    hf = hf * jnp.exp(-tn[:, None] * jnp.abs(deltas)[None, :])[:, None, None, :]
    fwd, bwd = hf[:, :, 0], hf[:, :, 1]
    taps = jnp.concatenate([fwd, jnp.zeros_like(fwd[:1]), jnp.flip(bwd[1:], axis=0)], axis=0)
    return jnp.fft.rfft(taps, axis=0)


def long_conv(u, kf, skip):
    L = u.shape[1]
    uf = u.astype(jnp.float32)
    y = jnp.fft.irfft(jnp.fft.rfft(uf, n=2 * L, axis=1) * kf[None], n=2 * L, axis=1)[:, :L]
    return (y + uf * skip.astype(jnp.float32)).astype(u.dtype)


def hyena_mixer(u, lp):
    L = u.shape[1]
    uc = dwconv_centred(u, lp['hy_conv_w'], HY_CONV // 2)
    v, x1, x2 = jnp.split(uc, 3, axis=-1)
    kf = hyena_filter_spectra(L, lp['hy_filt_w1'], lp['hy_filt_b1'], lp['hy_filt_w2'],
                              lp['hy_filt_b2'], lp['hy_filt_w3'], lp['hy_filt_b3'])
    z = x1 * long_conv(v, kf[:, 0], lp['hy_skip'][0])
    return x2 * long_conv(z, kf[:, 1], lp['hy_skip'][1])


def trunk_layer(x, mod, lp, ctx):
    sh1, sc1, g1, sh2, sc2, g2 = mod
    B, L, _ = x.shape
    h = rmsnorm(x, lp['norm1_w']) * (1 + sc1) + sh1
    p = h @ lp['w_in']
    q = rmsnorm(p[..., :OFF_K].reshape(B, L, N_HEADS, HEAD_DIM), lp['q_norm_w'])
    k = rmsnorm(p[..., OFF_K:OFF_V].reshape(B, L, N_KV, HEAD_DIM), lp['k_norm_w'])
    v = p[..., OFF_V:OFF_LX].reshape(B, L, N_KV, HEAD_DIM)
    if ctx is None:
        att = block_attention(q, k, v)
        h0f = jnp.zeros((B, LRU_W), jnp.float32)
        h0b = jnp.zeros((B, LRU_W), jnp.float32)
    else:
        k_ctx, v_ctx, s_ctx = ctx
        q = axial_rope(q)
        k = axial_rope(k)
        k_all = jnp.concatenate([k_ctx.astype(k.dtype), k], axis=1)
        v_all = jnp.concatenate([v_ctx.astype(v.dtype), v], axis=1)
        att = block_attention(q, k_all, v_all)
        h0f = s_ctx[:, 0].astype(jnp.float32)
        h0b = s_ctx[:, 1].astype(jnp.float32)
    xc = dwconv_centred(p[..., OFF_LX:OFF_LG], lp['lru_conv_w'], LRU_CONV // 2) + lp['lru_conv_b']
    a, b = rglru_coeffs(xc, lp['lru_gate_w'], lp['lru_gate_b'], lp['lru_lambda'])
    hf, hb = bidir_scan(a, b, h0f, h0b)
    lru = jax.nn.gelu(p[..., OFF_LG:OFF_HY]) * (hf + hb).astype(x.dtype)
    hy = hyena_mixer(p[..., OFF_HY:], lp)
    mix = jnp.concatenate([att, lru, hy], axis=-1)
    x = x + g1 * (mix @ lp['w_out'])
    h2 = rmsnorm(x, lp['norm2_w']) * (1 + sc2) + sh2
    x = x + g2 * ((jax.nn.silu(h2 @ lp['ffn_w1']) * (h2 @ lp['ffn_w3'])) @ lp['ffn_w2'])
    if ctx is None:
        state = jnp.stack([hf[:, -1], hb[:, 0]], axis=1).astype(x.dtype)
        return x, (k, v, state)
    return x, None


def setup_inputs(seed: int = 0) -> dict:
    key = jax.random.key(seed)
    ks = jax.random.split(key, 32)

    def nrm(k, shape, s):
        return s * jax.random.normal(k, shape, jnp.float32)

    a0 = jax.random.uniform(ks[18], (DEPTH, 2, LRU_W), jnp.float32, 0.9, 0.999) ** (1.0 / LRU_C)
    return {
        'x_prompt': nrm(ks[0], (BATCH, SEQ, D_MODEL), 1.0),
        'x_sample': nrm(ks[1], (DEC_BATCH, DEC_SEQ, D_MODEL), 1.0),
        'cache_k': nrm(ks[2], (DEC_BATCH, DEPTH, PAST_LEN, N_KV, HEAD_DIM), 1.0),
        'cache_v': nrm(ks[3], (DEC_BATCH, DEPTH, PAST_LEN, N_KV, HEAD_DIM), 1.0),
        'state_lru': nrm(ks[4], (DEC_BATCH, DEPTH, 2, LRU_W), 0.5),
        'c': nrm(ks[5], (DEC_BATCH, D_MODEL), 1.0),
        'c_ctx': nrm(ks[6], (D_MODEL,), 1.0),
        'w_mod': nrm(ks[7], (DEPTH, D_MODEL, 6 * D_MODEL), 0.5 * D_MODEL ** -0.5),
        'b_mod': nrm(ks[8], (DEPTH, 6 * D_MODEL), 0.01),
        'norm1_w': 1.0 + nrm(ks[9], (DEPTH, D_MODEL), 0.05),
        'norm2_w': 1.0 + nrm(ks[10], (DEPTH, D_MODEL), 0.05),
        'w_in': nrm(ks[11], (DEPTH, D_MODEL, D_IN), D_MODEL ** -0.5),
        'q_norm_w': 1.0 + nrm(ks[12], (DEPTH, HEAD_DIM), 0.05),
        'k_norm_w': 1.0 + nrm(ks[13], (DEPTH, HEAD_DIM), 0.05),
        'lru_conv_w': nrm(ks[14], (DEPTH, LRU_CONV, LRU_W), LRU_CONV ** -0.5),
        'lru_conv_b': nrm(ks[15], (DEPTH, LRU_W), 0.01),
        'lru_gate_w': nrm(ks[16], (DEPTH, 2, 2, LRU_BLOCKS, LRU_BS, LRU_BS), LRU_BS ** -0.5),
        'lru_gate_b': nrm(ks[17], (DEPTH, 2, 2, LRU_W), 0.1),
        'lru_lambda': jnp.log(a0) - jnp.log1p(-a0),
        'hy_conv_w': nrm(ks[19], (DEPTH, HY_CONV, 3 * HY_W), HY_CONV ** -0.5),
        'hy_filt_w1': nrm(ks[20], (DEPTH, HY_FEAT, HY_FILT_HID), HY_FEAT ** -0.5),
        'hy_filt_b1': nrm(ks[21], (DEPTH, HY_FILT_HID), 0.1),
        'hy_filt_w2': nrm(ks[22], (DEPTH, HY_FILT_HID, HY_FILT_HID), HY_FILT_HID ** -0.5),
        'hy_filt_b2': nrm(ks[23], (DEPTH, HY_FILT_HID), 0.1),
        'hy_filt_w3': nrm(ks[24], (DEPTH, HY_FILT_HID, 4 * HY_W), 0.05 * HY_FILT_HID ** -0.5),
        'hy_filt_b3': nrm(ks[25], (DEPTH, 4 * HY_W), 0.01),
        'hy_skip': nrm(ks[26], (DEPTH, 2, HY_W), 1.0),
        'w_out': nrm(ks[27], (DEPTH, D_MIX, D_MODEL), D_MIX ** -0.5),
        'ffn_w1': nrm(ks[28], (DEPTH, D_MODEL, D_FF), D_MODEL ** -0.5),
        'ffn_w3': nrm(ks[29], (DEPTH, D_MODEL, D_FF), D_MODEL ** -0.5),
        'ffn_w2': nrm(ks[30], (DEPTH, D_FF, D_MODEL), D_FF ** -0.5),
    }


def reference(x_prompt, x_sample, cache_k, cache_v, state_lru, c, c_ctx, w_mod, b_mod, norm1_w,
              norm2_w, w_in, q_norm_w, k_norm_w, lru_conv_w, lru_conv_b, lru_gate_w, lru_gate_b,
              lru_lambda, hy_conv_w, hy_filt_w1, hy_filt_b1, hy_filt_w2, hy_filt_b2, hy_filt_w3,
              hy_filt_b3, hy_skip, w_out, ffn_w1, ffn_w3, ffn_w2):
    y_p = x_prompt
    y_s = x_sample
    ks_new, vs_new, ss_new = [], [], []
    for l in range(DEPTH):
        lp = {
            'norm1_w': norm1_w[l], 'norm2_w': norm2_w[l], 'w_in': w_in[l],
            'q_norm_w': q_norm_w[l], 'k_norm_w': k_norm_w[l],
            'lru_conv_w': lru_conv_w[l], 'lru_conv_b': lru_conv_b[l],
            'lru_gate_w': lru_gate_w[l], 'lru_gate_b': lru_gate_b[l], 'lru_lambda': lru_lambda[l],
            'hy_conv_w': hy_conv_w[l], 'hy_filt_w1': hy_filt_w1[l], 'hy_filt_b1': hy_filt_b1[l],
            'hy_filt_w2': hy_filt_w2[l], 'hy_filt_b2': hy_filt_b2[l], 'hy_filt_w3': hy_filt_w3[l],
            'hy_filt_b3': hy_filt_b3[l], 'hy_skip': hy_skip[l], 'w_out': w_out[l],
            'ffn_w1': ffn_w1[l], 'ffn_w3': ffn_w3[l], 'ffn_w2': ffn_w2[l],
        }
        mod_ctx = adaln_mod(c_ctx[None, :], w_mod[l], b_mod[l])
        y_p, (k_l, v_l, s_l) = trunk_layer(y_p, mod_ctx, lp, None)
        ks_new.append(k_l)
        vs_new.append(v_l)
        ss_new.append(s_l)
        mod_lat = adaln_mod(c, w_mod[l], b_mod[l])
        y_s, _ = trunk_layer(y_s, mod_lat, lp, (cache_k[:, l], cache_v[:, l], state_lru[:, l]))
    new_k = jnp.stack(ks_new, axis=1)
    new_v = jnp.stack(vs_new, axis=1)
    new_state_lru = jnp.stack(ss_new, axis=1)
    return (y_p, y_s, new_k, new_v, new_state_lru)
```

```python
import functools
import math

import jax
import jax.numpy as jnp
import numpy as np
from jax.experimental import pallas as pl
from jax.experimental.pallas import tpu as pltpu

F32 = jnp.float32
BF16 = jnp.bfloat16

D_MODEL = 1024
DEPTH = 4
GRID_W = 64
HEAD_DIM = 64
ATT_W = 512
N_HEADS = 8
N_KV = 2
GQA_G = 4
KV_W = 128
ROPE_BASE = 10000.0
LRU_W = 256
LRU_BLOCKS = 4
LRU_BS = 64
LRU_C = 8.0
HY_W = 256
HY_BANDS = 16
HY_FEAT = 33
HY_FILT_HID = 64
HY_DECAY_FAST = 0.3
HY_DECAY_SLOW = 1.5
HY_DECAY_TARGET = 1e-2
D_FF = 2816
D_IN = 2048
EPS = 1e-6
N_MOD = 6 * D_MODEL

LANES = 128
COND_ROWS = 8
ROW_TILE = 512
FREQ_CHUNK = 256
FILT_PAD = 128
VMEM_LIMIT = 56 * 1024 * 1024


def _cparams(n_axes):
    return pltpu.CompilerParams(dimension_semantics=("arbitrary",) * n_axes,
                                vmem_limit_bytes=VMEM_LIMIT)


def _dot(a, b):
    return jnp.dot(a, b, preferred_element_type=F32)


def _dot_nt(a, b):
    return jax.lax.dot_general(a, b, (((1,), (1,)), ((), ())), preferred_element_type=F32)


def _split(x):
    hi = x.astype(BF16)
    lo = (x - hi.astype(F32)).astype(BF16)
    return hi, lo


def _dot3(ah, al, bh, bl):
    return _dot(ah, bh) + _dot(ah, bl) + _dot(al, bh)


def _rms_scale(x):
    return jax.lax.rsqrt(jnp.mean(x * x, axis=-1, keepdims=True) + EPS)


MOD_TN = 1536


def _mod_kernel(cond_ref, w_ref, b_ref, o_ref):
    cnd = cond_ref[...]
    s = cnd * jax.nn.sigmoid(cnd)
    sh, sl = _split(s)
    wh, wl = _split(w_ref[...])
    o_ref[...] = _dot3(sh, sl, wh, wl) + b_ref[...]


def _modulation(cond, w_mod, b_mod):
    return pl.pallas_call(
        _mod_kernel,
        grid=(DEPTH, N_MOD // MOD_TN),
        in_specs=[
            pl.BlockSpec((COND_ROWS, D_MODEL), lambda l, j: (0, 0)),
            pl.BlockSpec((None, D_MODEL, MOD_TN), lambda l, j: (l, 0, j)),
            pl.BlockSpec((None, 1, MOD_TN), lambda l, j: (l, 0, j)),
        ],
        out_specs=pl.BlockSpec((None, COND_ROWS, MOD_TN), lambda l, j: (l, 0, j)),
        out_shape=jax.ShapeDtypeStruct((DEPTH, COND_ROWS, N_MOD), F32),
        compiler_params=_cparams(2),
        name="adaln_mod",
    )(cond, w_mod, b_mod.reshape(DEPTH, 1, N_MOD))


def _in_kernel(x_ref, mod_ref, nw_ref, w_ref, p_ref):
    x = x_ref[...]
    sh1 = mod_ref[:, 0:D_MODEL]
    sc1 = mod_ref[:, D_MODEL:2 * D_MODEL]
    h = (x * _rms_scale(x) * nw_ref[...]) * (1.0 + sc1) + sh1
    p_ref[...] = _dot(h.astype(BF16), w_ref[...])


def _in_proj(x, mod, layer, rows_per_group, norm_w, w_in_b):
    n = x.shape[0]
    tiles_per_group = rows_per_group // ROW_TILE
    return pl.pallas_call(
        _in_kernel,
        grid=(n // ROW_TILE,),
        in_specs=[
            pl.BlockSpec((ROW_TILE, D_MODEL), lambda i: (i, 0)),
            pl.BlockSpec((None, 1, N_MOD), lambda i: (i // tiles_per_group, 0, 0)),
            pl.BlockSpec((1, D_MODEL), lambda i: (0, 0)),
            pl.BlockSpec((D_MODEL, D_IN), lambda i: (0, 0)),
        ],
        out_specs=pl.BlockSpec((ROW_TILE, D_IN), lambda i: (i, 0)),
        out_shape=jax.ShapeDtypeStruct((n, D_IN), F32),
        compiler_params=_cparams(1),
        name="in_proj",
    )(x, mod, norm_w, w_in_b)


ATT_QB = 256
Q_GROUP_W = GQA_G * HEAD_DIM


def _head_sum_matrix(width):
    idx = np.arange(width) // HEAD_DIM
    return jnp.asarray((idx[:, None] == idx[None, :]).astype(np.float32), dtype=BF16)


def _kv_tile_matrix():
    src = np.arange(KV_W)
    dst = np.arange(Q_GROUP_W)
    m = np.stack([(src[:, None] == (j * HEAD_DIM + dst[None, :] % HEAD_DIM)) for j in range(N_KV)])
    return jnp.asarray(m.astype(np.float32), dtype=BF16)


def _head_rmsnorm(x, w, gsum):
    sq = x * x
    hi, lo = _split(sq)
    ms = (_dot(hi, gsum) + _dot(lo, gsum)) * (1.0 / HEAD_DIM)
    return x * jax.lax.rsqrt(ms + EPS) * w


def _rope(x, cos, sin):
    w = x.shape[-1]
    lane = jax.lax.broadcasted_iota(jnp.int32, x.shape, 1)
    up = pltpu.roll(x, w - HEAD_DIM // 4, axis=1)
    down = pltpu.roll(x, HEAD_DIM // 4, axis=1)
    partner = jnp.where((lane % (HEAD_DIM // 2)) < HEAD_DIM // 4, up, down)
    return x * cos + partner * sin


def _attend(qn, kexp_ref, vexp_ref, o_ref):
    rows = qn.shape[0]
    qb = qn.astype(BF16)
    lane = jax.lax.broadcasted_iota(jnp.int32, (rows, Q_GROUP_W), 1)
    for j in range(N_KV):
        qj = qb[:, j * Q_GROUP_W:(j + 1) * Q_GROUP_W]
        kx = kexp_ref[j]
        vx = vexp_ref[j]
        out = jnp.zeros((rows, Q_GROUP_W), F32)
        for g in range(GQA_G):
            mask = (lane // HEAD_DIM) == g
            qm = jnp.where(mask, qj, jnp.zeros_like(qj))
            s = _dot_nt(qm, kx) * (HEAD_DIM ** -0.5)
            m = jnp.max(s, axis=-1, keepdims=True)
            e = jnp.exp(s - m)
            inv = 1.0 / jnp.sum(e, axis=-1, keepdims=True)
            og = _dot(e.astype(BF16), vx) * inv
            out = jnp.where(mask, og, out)
        o_ref[:, j * Q_GROUP_W:(j + 1) * Q_GROUP_W] = out


def _attn_ctx_kernel(q_ref, kv_ref, qw_ref, kw_ref, gq_ref, gk_ref, tile_ref,
                     o_ref, kn_ref, v_ref, kexp, vexp):
    kv = kv_ref[...]
    kn = _head_rmsnorm(kv[:, :KV_W], kw_ref[...], gk_ref[...])
    v = kv[:, KV_W:]
    kn_ref[...] = kn
    v_ref[...] = v
    kb = kn.astype(BF16)
    vb = v.astype(BF16)
    for j in range(N_KV):
        kexp[j] = _dot(kb, tile_ref[j]).astype(BF16)
        vexp[j] = _dot(vb, tile_ref[j]).astype(BF16)
    qn = _head_rmsnorm(q_ref[...], qw_ref[...], gq_ref[...])
    _attend(qn, kexp, vexp, o_ref)


def _attn_ctx(p, n_batch, seq, qw, kw):
    n = p.shape[0]
    return pl.pallas_call(
        _attn_ctx_kernel,
        grid=(n_batch,),
        in_specs=[
            pl.BlockSpec((seq, ATT_W), lambda b: (b, 0)),
            pl.BlockSpec((seq, 2 * KV_W), lambda b: (b, ATT_W // (2 * KV_W))),
            pl.BlockSpec((1, ATT_W), lambda b: (0, 0)),
            pl.BlockSpec((1, KV_W), lambda b: (0, 0)),
            pl.BlockSpec((ATT_W, ATT_W), lambda b: (0, 0)),
            pl.BlockSpec((KV_W, KV_W), lambda b: (0, 0)),
            pl.BlockSpec((N_KV, KV_W, Q_GROUP_W), lambda b: (0, 0, 0)),
        ],
        out_specs=[
            pl.BlockSpec((seq, ATT_W), lambda b: (b, 0)),
            pl.BlockSpec((seq, KV_W), lambda b: (b, 0)),
            pl.BlockSpec((seq, KV_W), lambda b: (b, 0)),
        ],
        out_shape=[
            jax.ShapeDtypeStruct((n, ATT_W), F32),
            jax.ShapeDtypeStruct((n, KV_W), F32),
            jax.ShapeDtypeStruct((n, KV_W), F32),
        ],
        scratch_shapes=[pltpu.VMEM((N_KV, seq, Q_GROUP_W), BF16),
                        pltpu.VMEM((N_KV, seq, Q_GROUP_W), BF16)],
        compiler_params=_cparams(1),
        name="attn_ctx",
    )(p, p, qw, kw, _head_sum_matrix(ATT_W), _head_sum_matrix(KV_W), _kv_tile_matrix())


def _attn_lat_kernel(q_ref, kv_ref, ck_ref, cv_ref, qw_ref, kw_ref, gq_ref, gk_ref, tile_ref,
                     cq_ref, sq_ref, ck_rope_ref, sk_rope_ref, o_ref, kexp, vexp, *, past, seq):
    @pl.when(pl.program_id(1) == 0)
    def _():
        kv = kv_ref[...]
        kn = _head_rmsnorm(kv[:, :KV_W], kw_ref[...], gk_ref[...])
        kn = _rope(kn, ck_rope_ref[...], sk_rope_ref[...])
        kb = kn.astype(BF16)
        vb = kv[:, KV_W:].astype(BF16)
        ckb = ck_ref[...].astype(BF16)
        cvb = cv_ref[...].astype(BF16)
        for j in range(N_KV):
            kexp[j, 0:past, :] = _dot(ckb, tile_ref[j]).astype(BF16)
            vexp[j, 0:past, :] = _dot(cvb, tile_ref[j]).astype(BF16)
            kexp[j, past:past + seq, :] = _dot(kb, tile_ref[j]).astype(BF16)
            vexp[j, past:past + seq, :] = _dot(vb, tile_ref[j]).astype(BF16)

    qn = _head_rmsnorm(q_ref[...], qw_ref[...], gq_ref[...])
    qn = _rope(qn, cq_ref[...], sq_ref[...])
    _attend(qn, kexp, vexp, o_ref)


def _rope_tables(seq, width):
    rows = seq // GRID_W
    row = jnp.repeat(jnp.arange(rows, dtype=F32), GRID_W)
    col = jnp.tile(jnp.arange(GRID_W, dtype=F32), rows)
    nq = HEAD_DIM // 4
    freqs = ROPE_BASE ** (-jnp.arange(nq, dtype=F32) / nq)
    ar = row[:, None] * freqs[None, :]
    ac = col[:, None] * freqs[None, :]
    cos = jnp.concatenate([jnp.cos(ar), jnp.cos(ar), jnp.cos(ac), jnp.cos(ac)], axis=-1)
    sin = jnp.concatenate([-jnp.sin(ar), jnp.sin(ar), -jnp.sin(ac), jnp.sin(ac)], axis=-1)
    reps = width // HEAD_DIM
    return jnp.tile(cos, (1, reps)), jnp.tile(sin, (1, reps))


def _attn_lat(p, cache_k, cache_v, layer, n_batch, seq, past, qw, kw, tables):
    n = p.shape[0]
    nqb = seq // ATT_QB
    cq, sq, ck, sk = tables
    lk = past + seq
    return pl.pallas_call(
        functools.partial(_attn_lat_kernel, past=past, seq=seq),
        grid=(n_batch, nqb),
        in_specs=[
            pl.BlockSpec((ATT_QB, ATT_W), lambda b, i: (b * nqb + i, 0)),
            pl.BlockSpec((seq, 2 * KV_W), lambda b, i: (b, ATT_W // (2 * KV_W))),
            pl.BlockSpec((None, None, past, KV_W), lambda b, i: (b, layer, 0, 0)),
            pl.BlockSpec((None, None, past, KV_W), lambda b, i: (b, layer, 0, 0)),
            pl.BlockSpec((1, ATT_W), lambda b, i: (0, 0)),
            pl.BlockSpec((1, KV_W), lambda b, i: (0, 0)),
            pl.BlockSpec((ATT_W, ATT_W), lambda b, i: (0, 0)),
            pl.BlockSpec((KV_W, KV_W), lambda b, i: (0, 0)),
            pl.BlockSpec((N_KV, KV_W, Q_GROUP_W), lambda b, i: (0, 0, 0)),
            pl.BlockSpec((ATT_QB, ATT_W), lambda b, i: (i, 0)),
            pl.BlockSpec((ATT_QB, ATT_W), lambda b, i: (i, 0)),
            pl.BlockSpec((seq, KV_W), lambda b, i: (0, 0)),
            pl.BlockSpec((seq, KV_W), lambda b, i: (0, 0)),
        ],
        out_specs=pl.BlockSpec((ATT_QB, ATT_W), lambda b, i: (b * nqb + i, 0)),
        out_shape=jax.ShapeDtypeStruct((n, ATT_W), F32),
        scratch_shapes=[pltpu.VMEM((N_KV, lk, Q_GROUP_W), BF16),
                        pltpu.VMEM((N_KV, lk, Q_GROUP_W), BF16)],
        compiler_params=_cparams(2),
        name="attn_lat",
    )(p, p, cache_k, cache_v, qw, kw, _head_sum_matrix(ATT_W), _head_sum_matrix(KV_W),
      _kv_tile_matrix(), cq, sq, ck, sk)


def _shift_rows(x, d, fill):
    n = x.shape[0]
    t = jax.lax.broadcasted_iota(jnp.int32, x.shape, 0)
    r = pltpu.roll(x, d % n, axis=0)
    valid = (t >= d) if d > 0 else (t < n + d)
    return jnp.where(valid, r, jnp.full_like(r, fill))


def _scan(a, b, h0, reverse):
    n = a.shape[0]
    d = 1
    while d < n:
        s = -d if reverse else d
        b = a * _shift_rows(b, s, 0.0) + b
        a = a * _shift_rows(a, s, 1.0)
        d *= 2
    return a * h0 + b


def _gelu_tanh(x):
    return 0.5 * x * (1.0 + jnp.tanh(math.sqrt(2.0 / math.pi) * (x + 0.044715 * (x * x * x))))


def _lru_kernel(lx_ref, lg_ref, cw_ref, cb_ref, gw_ref, gb_ref, lam_ref, h0_ref, o_ref, st_ref):
    lx = lx_ref[...]
    cw = cw_ref[...]
    xc = (cw[0:1] * _shift_rows(lx, 2, 0.0) + cw[1:2] * _shift_rows(lx, 1, 0.0)
          + cw[2:3] * lx + cw[3:4] * _shift_rows(lx, -1, 0.0)) + cb_ref[...]
    xh, xl = _split(xc)
    gh, gl = _split(gw_ref[...])
    gates = _dot3(xh, xl, gh, gl) + gb_ref[...]
    lam = lam_ref[...]
    neg = -lam
    softplus = jnp.maximum(neg, 0.0) + jnp.log1p(jnp.exp(-jnp.abs(neg)))
    h0 = h0_ref[...]
    total = None
    for dirn in range(2):
        r = jax.nn.sigmoid(gates[:, (2 * dirn) * LRU_W:(2 * dirn + 1) * LRU_W])
        i = jax.nn.sigmoid(gates[:, (2 * dirn + 1) * LRU_W:(2 * dirn + 2) * LRU_W])
        log_a = -LRU_C * r * softplus[dirn:dirn + 1]
        a = jnp.exp(log_a)
        mult = jnp.sqrt(-jnp.tanh(log_a) * (a * a + 1.0))
        b = mult * (i * xc)
        h = _scan(a, b, h0[dirn:dirn + 1], reverse=(dirn == 1))
        n = h.shape[0]
        st_ref[dirn:dirn + 1, :] = h[n - 1:n] if dirn == 0 else h[0:1]
        total = h if total is None else total + h
    o_ref[...] = _gelu_tanh(lg_ref[...]) * total


def _lru(p, n_batch, seq, cw, cb, gw, gb, lam, h0):
    n = p.shape[0]
    lx_blk = 768 // LRU_W
    return pl.pallas_call(
        _lru_kernel,
        grid=(n_batch,),
        in_specs=[
            pl.BlockSpec((seq, LRU_W), lambda b: (b, lx_blk)),
            pl.BlockSpec((seq, LRU_W), lambda b: (b, lx_blk + 1)),
            pl.BlockSpec((4, LRU_W), lambda b: (0, 0)),
            pl.BlockSpec((1, LRU_W), lambda b: (0, 0)),
            pl.BlockSpec((LRU_W, 4 * LRU_W), lambda b: (0, 0)),
            pl.BlockSpec((1, 4 * LRU_W), lambda b: (0, 0)),
            pl.BlockSpec((2, LRU_W), lambda b: (0, 0)),
            pl.BlockSpec((None, 2, LRU_W), lambda b: (b, 0, 0)),
        ],
        out_specs=[
            pl.BlockSpec((seq, LRU_W), lambda b: (b, 0)),
            pl.BlockSpec((None, 2, LRU_W), lambda b: (b, 0, 0)),
        ],
        out_shape=[
            jax.ShapeDtypeStruct((n, LRU_W), F32),
            jax.ShapeDtypeStruct((n_batch, 2, LRU_W), F32),
        ],
        compiler_params=_cparams(1),
        name="rglru",
    )(p, p, cw, cb, gw, gb, lam, h0)


def _gate_matrix(gate_w):
    eye = jnp.eye(LRU_BLOCKS, dtype=gate_w.dtype)
    dense = jnp.einsum('dgnij,nm->dgnimj', gate_w, eye).reshape(2, 2, LRU_W, LRU_W)
    return dense.transpose(2, 0, 1, 3).reshape(LRU_W, 4 * LRU_W)


def _dft_matrices(seq):
    n2 = 2 * seq
    f = jnp.arange(seq, dtype=jnp.int32)
    t = jnp.arange(seq, dtype=jnp.int32)
    ang = ((f[:, None] * t[None, :]) % n2).astype(F32) * (2.0 * math.pi / n2)
    cos = jnp.cos(ang)
    sin = jnp.sin(ang)
    alt = jnp.where(t % 2 == 0, 1.0, -1.0).astype(F32)
    f0 = (f == 0)[:, None]
    fwd_re = cos
    fwd_im = jnp.where(f0, alt[None, :], -sin)
    wgt = jnp.where(f0, 1.0, 2.0) / n2
    inv_re = (cos * wgt).T
    inv_im = jnp.where(f0, alt[None, :] / n2, -sin * wgt).T
    nchunk = seq // FREQ_CHUNK
    fwd = jnp.concatenate([fwd_re.reshape(nchunk, FREQ_CHUNK, seq), fwd_im.reshape(nchunk, FREQ_CHUNK, seq)], axis=1)
    inv = jnp.concatenate([inv_re.reshape(seq, nchunk, FREQ_CHUNK), inv_im.reshape(seq, nchunk, FREQ_CHUNK)], axis=2)
    inv = inv.transpose(1, 0, 2)
    return _split(fwd) + _split(inv)


def _filter_features(seq):
    t = jnp.arange(seq, dtype=F32)
    tn = t / seq
    bands = jnp.linspace(1e-4, HY_BANDS - 1, HY_BANDS, dtype=F32)
    ang = (2.0 * math.pi / seq) * t[:, None] * bands[None, :]
    z = jnp.concatenate([tn[:, None], jnp.cos(ang), -jnp.sin(ang)], axis=-1)
    return jnp.pad(z, ((0, 0), (0, FILT_PAD - HY_FEAT)))


def _filter_kernel(z_ref, w1_ref, b1_ref, w2_ref, b2_ref, w3_ref, b3_ref, dl_ref, fh_ref, fl_ref,
                   kr_ref, ki_ref, hf_hi, hf_lo):
    c = pl.program_id(1)

    @pl.when(c == 0)
    def _():
        z = z_ref[...]
        zh, zl = _split(z)
        w1h, w1l = _split(w1_ref[...])
        h = jnp.sin(_dot3(zh, zl, w1h, w1l) + b1_ref[...])
        hh, hl = _split(h)
        w2h, w2l = _split(w2_ref[...])
        h = jnp.sin(_dot3(hh, hl, w2h, w2l) + b2_ref[...])
        hh, hl = _split(h)
        w3h, w3l = _split(w3_ref[...])
        hf = _dot3(hh, hl, w3h, w3l) + b3_ref[...]
        decay = jnp.exp(-z[:, 0:1] * jnp.abs(dl_ref[...]))
        row = jax.lax.broadcasted_iota(jnp.int32, decay.shape, 0)
        for o in range(2):
            for s in range(2):
                blk = hf[:, (2 * o + s) * HY_W:(2 * o + s + 1) * HY_W] * decay
                if s == 1:
                    blk = jnp.where(row == 0, 0.0, blk)
                bh, bl = _split(blk)
                hf_hi[:, (2 * o + s) * HY_W:(2 * o + s + 1) * HY_W] = bh
                hf_lo[:, (2 * o + s) * HY_W:(2 * o + s + 1) * HY_W] = bl

    spec = _dot3(fh_ref[...], fl_ref[...], hf_hi[...], hf_lo[...])
    re = spec[:FREQ_CHUNK]
    im = spec[FREQ_CHUNK:]
    row = jax.lax.broadcasted_iota(jnp.int32, (FREQ_CHUNK, HY_W), 0)
    nyq = jnp.logical_and(row == 0, c == 0)
    for o in range(2):
        a = slice((2 * o) * HY_W, (2 * o + 1) * HY_W)
        b = slice((2 * o + 1) * HY_W, (2 * o + 2) * HY_W)
        kr_ref[o] = re[:, a] + re[:, b]
        ki_ref[o] = jnp.where(nyq, im[:, a] + im[:, b], im[:, a] - im[:, b])


def _filter_spectra(seq, w1p, b1p, w2p, b2p, w3p, b3, fwd_h, fwd_l):
    nchunk = seq // FREQ_CHUNK
    deltas = jnp.linspace(math.log(HY_DECAY_TARGET) / HY_DECAY_SLOW,
                          math.log(HY_DECAY_TARGET) / HY_DECAY_FAST, HY_W, dtype=F32).reshape(1, HY_W)
    out = jax.ShapeDtypeStruct((DEPTH, 2, nchunk, FREQ_CHUNK, HY_W), F32)
    return pl.pallas_call(
        _filter_kernel,
        grid=(DEPTH, nchunk),
        in_specs=[
            pl.BlockSpec((seq, FILT_PAD), lambda l, c: (0, 0)),
            pl.BlockSpec((None, FILT_PAD, FILT_PAD), lambda l, c: (l, 0, 0)),
            pl.BlockSpec((None, 1, FILT_PAD), lambda l, c: (l, 0, 0)),
            pl.BlockSpec((None, FILT_PAD, FILT_PAD), lambda l, c: (l, 0, 0)),
            pl.BlockSpec((None, 1, FILT_PAD), lambda l, c: (l, 0, 0)),
            pl.BlockSpec((None, FILT_PAD, 4 * HY_W), lambda l, c: (l, 0, 0)),
            pl.BlockSpec((None, 1, 4 * HY_W), lambda l, c: (l, 0, 0)),
            pl.BlockSpec((1, HY_W), lambda l, c: (0, 0)),
            pl.BlockSpec((None, 2 * FREQ_CHUNK, seq), lambda l, c: (c, 0, 0)),
            pl.BlockSpec((None, 2 * FREQ_CHUNK, seq), lambda l, c: (c, 0, 0)),
        ],
        out_specs=[
            pl.BlockSpec((None, 2, None, FREQ_CHUNK, HY_W), lambda l, c: (l, 0, c, 0, 0)),
            pl.BlockSpec((None, 2, None, FREQ_CHUNK, HY_W), lambda l, c: (l, 0, c, 0, 0)),
        ],
        out_shape=[out, out],
        scratch_shapes=[pltpu.VMEM((seq, 4 * HY_W), BF16), pltpu.VMEM((seq, 4 * HY_W), BF16)],
        compiler_params=_cparams(2),
        name="hyena_filter",
    )(_filter_features(seq), w1p, b1p, w2p, b2p, w3p, b3, deltas, fwd_h, fwd_l)


def _hyena_kernel(v_ref, x1_ref, x2_ref, cw_ref, skip_ref, kr_ref, ki_ref, fh_ref, fl_ref, ih_ref, il_ref,
                  o_ref, u_hi, u_lo, u_f32, x1c, x2c, acc, *, nchunk):
    o = pl.program_id(1)
    c = pl.program_id(2)

    def conv3(x, k):
        cw = cw_ref[...]
        w = cw[:, k * HY_W:(k + 1) * HY_W]
        return w[0:1] * _shift_rows(x, 1, 0.0) + w[1:2] * x + w[2:3] * _shift_rows(x, -1, 0.0)

    def set_u(u):
        u_f32[...] = u
        h, l = _split(u)
        u_hi[...] = h
        u_lo[...] = l

    @pl.when(jnp.logical_and(o == 0, c == 0))
    def _():
        set_u(conv3(v_ref[...], 0))
        x1c[...] = conv3(x1_ref[...], 1)
        x2c[...] = conv3(x2_ref[...], 2)

    spec = _dot3(fh_ref[...], fl_ref[...], u_hi[...], u_lo[...])
    ur = spec[:FREQ_CHUNK]
    ui = spec[FREQ_CHUNK:]
    kr = kr_ref[...]
    ki = ki_ref[...]
    row = jax.lax.broadcasted_iota(jnp.int32, (FREQ_CHUNK, HY_W), 0)
    packed = jnp.logical_and(row == 0, c == 0)
    ki_x = jnp.where(packed, 0.0, ki)
    kr_i = jnp.where(packed, ki, kr)
    yr = ur * kr - ui * ki_x
    yi = ur * ki_x + ui * kr_i
    yh, yl = _split(jnp.concatenate([yr, yi], axis=0))
    part = _dot3(ih_ref[...], il_ref[...], yh, yl)

    @pl.when(c == 0)
    def _():
        acc[...] = part

    @pl.when(c != 0)
    def _():
        acc[...] += part

    @pl.when(jnp.logical_and(c == nchunk - 1, o == 0))
    def _():
        u = u_f32[...]
        set_u(x1c[...] * (acc[...] + u * skip_ref[0:1, :]))

    @pl.when(jnp.logical_and(c == nchunk - 1, o == 1))
    def _():
        u = u_f32[...]
        o_ref[...] = x2c[...] * (acc[...] + u * skip_ref[1:2, :])


def _hyena(p, n_batch, seq, layer, cw, skip, kr, ki, dft):
    n = p.shape[0]
    nchunk = seq // FREQ_CHUNK
    fh, fl, ih, il = dft
    blk0 = 1280 // HY_W
    vm = lambda shape, dt: pltpu.VMEM(shape, dt)
    return pl.pallas_call(
        functools.partial(_hyena_kernel, nchunk=nchunk),
        grid=(n_batch, 2, nchunk),
        in_specs=[
            pl.BlockSpec((seq, HY_W), lambda b, o, c: (b, blk0)),
            pl.BlockSpec((seq, HY_W), lambda b, o, c: (b, blk0 + 1)),
            pl.BlockSpec((seq, HY_W), lambda b, o, c: (b, blk0 + 2)),
            pl.BlockSpec((3, 3 * HY_W), lambda b, o, c: (0, 0)),
            pl.BlockSpec((2, HY_W), lambda b, o, c: (0, 0)),
            pl.BlockSpec((None, None, None, FREQ_CHUNK, HY_W), lambda b, o, c: (layer, o, c, 0, 0)),
            pl.BlockSpec((None, None, None, FREQ_CHUNK, HY_W), lambda b, o, c: (layer, o, c, 0, 0)),
            pl.BlockSpec((None, 2 * FREQ_CHUNK, seq), lambda b, o, c: (c, 0, 0)),
            pl.BlockSpec((None, 2 * FREQ_CHUNK, seq), lambda b, o, c: (c, 0, 0)),
            pl.BlockSpec((None, seq, 2 * FREQ_CHUNK), lambda b, o, c: (c, 0, 0)),
            pl.BlockSpec((None, seq, 2 * FREQ_CHUNK), lambda b, o, c: (c, 0, 0)),
        ],
        out_specs=pl.BlockSpec((seq, HY_W), lambda b, o, c: (b, 0)),
        out_shape=jax.ShapeDtypeStruct((n, HY_W), F32),
        scratch_shapes=[vm((seq, HY_W), BF16), vm((seq, HY_W), BF16), vm((seq, HY_W), F32),
                        vm((seq, HY_W), F32), vm((seq, HY_W), F32), vm((seq, HY_W), F32)],
        compiler_params=_cparams(3),
        name="hyena",
    )(p, p, p, cw, skip, kr, ki, fh, fl, ih, il)


FF_CHUNK = 1408


def _out_ffn_kernel(x_ref, att_ref, lru_ref, hy_ref, mod_ref, nw_ref, wo_ref, w1_ref, w3_ref, w2_ref,
                    o_ref, t_scr):
    g1 = mod_ref[:, 2 * D_MODEL:3 * D_MODEL]
    sh2 = mod_ref[:, 3 * D_MODEL:4 * D_MODEL]
    sc2 = mod_ref[:, 4 * D_MODEL:5 * D_MODEL]
    g2 = mod_ref[:, 5 * D_MODEL:6 * D_MODEL]
    mix = jnp.concatenate([att_ref[...], lru_ref[...], hy_ref[...]], axis=1).astype(BF16)
    x1 = x_ref[...] + g1 * _dot(mix, wo_ref[...])
    h2 = ((x1 * _rms_scale(x1) * nw_ref[...]) * (1.0 + sc2) + sh2).astype(BF16)
    for k in range(D_FF // FF_CHUNK):
        cols = slice(k * FF_CHUNK, (k + 1) * FF_CHUNK)
        a = _dot(h2, w1_ref[:, cols])
        b = _dot(h2, w3_ref[:, cols])
        t_scr[:, cols] = ((a * jax.nn.sigmoid(a)) * b).astype(BF16)
    o_ref[...] = x1 + g2 * _dot(t_scr[...], w2_ref[...])


def _resident(shape):
    return pl.BlockSpec(shape, lambda i: (0,) * len(shape), pipeline_mode=pl.Buffered(1))


def _out_ffn(x, att, lru, hy, mod, rows_per_group, norm_w, wo_b, w1_b, w3_b, w2_b):
    n = x.shape[0]
    tiles_per_group = rows_per_group // ROW_TILE
    return pl.pallas_call(
        _out_ffn_kernel,
        grid=(n // ROW_TILE,),
        in_specs=[
            pl.BlockSpec((ROW_TILE, D_MODEL), lambda i: (i, 0)),
            pl.BlockSpec((ROW_TILE, ATT_W), lambda i: (i, 0)),
            pl.BlockSpec((ROW_TILE, LRU_W), lambda i: (i, 0)),
            pl.BlockSpec((ROW_TILE, HY_W), lambda i: (i, 0)),
            pl.BlockSpec((None, 1, N_MOD), lambda i: (i // tiles_per_group, 0, 0)),
            pl.BlockSpec((1, D_MODEL), lambda i: (0, 0)),
            _resident((D_MODEL, D_MODEL)),
            _resident((D_MODEL, D_FF)),
            _resident((D_MODEL, D_FF)),
            _resident((D_FF, D_MODEL)),
        ],
        out_specs=pl.BlockSpec((ROW_TILE, D_MODEL), lambda i: (i, 0)),
        out_shape=jax.ShapeDtypeStruct((n, D_MODEL), F32),
        scratch_shapes=[pltpu.VMEM((ROW_TILE, D_FF), BF16)],
        compiler_params=_cparams(1),
        name="out_ffn",
    )(x, att, lru, hy, mod, norm_w, wo_b, w1_b, w3_b, w2_b)


def kernel(x_prompt, x_sample, cache_k, cache_v, state_lru, c, c_ctx, w_mod, b_mod, norm1_w, norm2_w, w_in, q_norm_w, k_norm_w, lru_conv_w, lru_conv_b, lru_gate_w, lru_gate_b, lru_lambda, hy_conv_w, hy_filt_w1, hy_filt_b1, hy_filt_w2, hy_filt_b2, hy_filt_w3, hy_filt_b3, hy_skip, w_out, ffn_w1, ffn_w3, ffn_w2):
    batch, seq, _ = x_prompt.shape
    dec_batch, dec_seq, _ = x_sample.shape
    past = cache_k.shape[2]
    assert 1 + dec_batch <= COND_ROWS

    cond = jnp.concatenate([c_ctx[None, :], c, jnp.zeros((COND_ROWS - 1 - dec_batch, D_MODEL), F32)], axis=0)
    mod = _modulation(cond, w_mod, b_mod)
    mod = mod.reshape(DEPTH, COND_ROWS, 1, N_MOD)

    w_in_b = w_in.astype(BF16)
    w_out_b = w_out.astype(BF16)
    w1_b = ffn_w1.astype(BF16)
    w3_b = ffn_w3.astype(BF16)
    w2_b = ffn_w2.astype(BF16)
    qw = jnp.tile(q_norm_w, (1, N_HEADS)).reshape(DEPTH, 1, ATT_W)
    kw = jnp.tile(k_norm_w, (1, N_KV)).reshape(DEPTH, 1, KV_W)
    gate_w = jax.vmap(_gate_matrix)(lru_gate_w)
    gate_b = lru_gate_b.reshape(DEPTH, 1, 4 * LRU_W)
    pad_f = FILT_PAD - HY_FEAT
    pad_h = FILT_PAD - HY_FILT_HID
    w1p = jnp.pad(hy_filt_w1, ((0, 0), (0, pad_f), (0, pad_h)))
    b1p = jnp.pad(hy_filt_b1, ((0, 0), (0, pad_h))).reshape(DEPTH, 1, FILT_PAD)
    w2p = jnp.pad(hy_filt_w2, ((0, 0), (0, pad_h), (0, pad_h)))
    b2p = jnp.pad(hy_filt_b2, ((0, 0), (0, pad_h))).reshape(DEPTH, 1, FILT_PAD)
    w3p = jnp.pad(hy_filt_w3, ((0, 0), (0, pad_h), (0, 0)))
    b3 = hy_filt_b3.reshape(DEPTH, 1, 4 * HY_W)

    dft_ctx = _dft_matrices(seq)
    dft_lat = _dft_matrices(dec_seq)
    kr_ctx, ki_ctx = _filter_spectra(seq, w1p, b1p, w2p, b2p, w3p, b3, dft_ctx[0], dft_ctx[1])
    kr_lat, ki_lat = _filter_spectra(dec_seq, w1p, b1p, w2p, b2p, w3p, b3, dft_lat[0], dft_lat[1])
    rope = _rope_tables(dec_seq, ATT_W) + _rope_tables(dec_seq, KV_W)

    ck = cache_k.reshape(dec_batch, DEPTH, past, KV_W)
    cv = cache_v.reshape(dec_batch, DEPTH, past, KV_W)
    zero_state = jnp.zeros((batch, 2, LRU_W), F32)

    y_p = x_prompt.reshape(batch * seq, D_MODEL)
    y_s = x_sample.reshape(dec_batch * dec_seq, D_MODEL)
    ks, vs, ss = [], [], []
    for l in range(DEPTH):
        norm1 = norm1_w[l].reshape(1, D_MODEL)
        norm2 = norm2_w[l].reshape(1, D_MODEL)
        lru_args = (lru_conv_w[l], lru_conv_b[l].reshape(1, LRU_W), gate_w[l], gate_b[l], lru_lambda[l])
        mod_ctx = mod[l, 0:1]
        mod_lat = mod[l, 1:1 + dec_batch]

        p = _in_proj(y_p, mod_ctx, l, batch * seq, norm1, w_in_b[l])
        att, kn, v = _attn_ctx(p, batch, seq, qw[l], kw[l])
        lru, st = _lru(p, batch, seq, *lru_args, zero_state)
        hy = _hyena(p, batch, seq, l, hy_conv_w[l], hy_skip[l], kr_ctx, ki_ctx, dft_ctx)
        y_p = _out_ffn(y_p, att, lru, hy, mod_ctx, batch * seq, norm2, w_out_b[l], w1_b[l], w3_b[l], w2_b[l])
        ks.append(kn.reshape(batch, seq, N_KV, HEAD_DIM))
        vs.append(v.reshape(batch, seq, N_KV, HEAD_DIM))
        ss.append(st)

        p = _in_proj(y_s, mod_lat, l, dec_seq, norm1, w_in_b[l])
        att = _attn_lat(p, ck, cv, l, dec_batch, dec_seq, past, qw[l], kw[l], rope)
        lru, _ = _lru(p, dec_batch, dec_seq, *lru_args, state_lru[:, l])
        hy = _hyena(p, dec_batch, dec_seq, l, hy_conv_w[l], hy_skip[l], kr_lat, ki_lat, dft_lat)
        y_s = _out_ffn(y_s, att, lru, hy, mod_lat, dec_seq, norm2, w_out_b[l], w1_b[l], w3_b[l], w2_b[l])

    return (y_p.reshape(batch, seq, D_MODEL), y_s.reshape(dec_batch, dec_seq, D_MODEL),
            jnp.stack(ks, axis=1), jnp.stack(vs, axis=1), jnp.stack(ss, axis=1))
```

```python
import functools
import math

import jax
import jax.numpy as jnp
import numpy as np
from jax.experimental import pallas as pl
from jax.experimental.pallas import tpu as pltpu

F32 = jnp.float32
BF16 = jnp.bfloat16

D_MODEL = 1024
DEPTH = 4
GRID_W = 64
HEAD_DIM = 64
ATT_W = 512
N_HEADS = 8
N_KV = 2
GQA_G = 4
KV_W = 128
ROPE_BASE = 10000.0
LRU_W = 256
LRU_BLOCKS = 4
LRU_C = 8.0
HY_W = 256
HY_BANDS = 16
HY_FEAT = 33
HY_FILT_HID = 64
HY_DECAY_FAST = 0.3
HY_DECAY_SLOW = 1.5
HY_DECAY_TARGET = 1e-2
D_FF = 2816
EPS = 1e-6
N_MOD = 6 * D_MODEL

OFF_KV = ATT_W
OFF_LX = OFF_KV + 2 * KV_W
OFF_LG = OFF_LX + LRU_W
OFF_HY = OFF_LG + LRU_W
D_IN = OFF_HY + 3 * HY_W

LANES = 128
SUBLANES = 8
COND_ROWS = SUBLANES
ROW_TILE = 512
FREQ_CHUNK = 256
FILT_PAD = LANES
VMEM_LIMIT = 56 * 1024 * 1024


def _cparams(n_axes):
    return pltpu.CompilerParams(dimension_semantics=("arbitrary",) * n_axes,
                                vmem_limit_bytes=VMEM_LIMIT)


def _layer_spec(shape, layer, n_grid):
    zeros = (0,) * len(shape)
    if n_grid == 1:
        return pl.BlockSpec((None,) + shape, lambda i: (layer,) + zeros)
    if n_grid == 2:
        return pl.BlockSpec((None,) + shape, lambda i, j: (layer,) + zeros)
    return pl.BlockSpec((None,) + shape, lambda i, j, k: (layer,) + zeros)


def _const_spec(shape, n_grid):
    zeros = (0,) * len(shape)
    if n_grid == 1:
        return pl.BlockSpec(shape, lambda i: zeros)
    if n_grid == 2:
        return pl.BlockSpec(shape, lambda i, j: zeros)
    return pl.BlockSpec(shape, lambda i, j, k: zeros)


def _dot(a, b):
    return jnp.dot(a, b, preferred_element_type=F32)


def _dot_nt(a, b):
    return jax.lax.dot_general(a, b, (((1,), (1,)), ((), ())), preferred_element_type=F32)


def _split(x):
    hi = x.astype(BF16)
    lo = (x - hi.astype(F32)).astype(BF16)
    return hi, lo


def _dot3(ah, al, bh, bl):
    return _dot(ah, bh) + _dot(ah, bl) + _dot(al, bh)


def _rms_scale(x):
    return jax.lax.rsqrt(jnp.mean(x * x, axis=-1, keepdims=True) + EPS)


MOD_TN = 1536


def _mod_kernel(cond_ref, w_ref, b_ref, o_ref):
    cnd = cond_ref[...]
    s = cnd * jax.nn.sigmoid(cnd)
    sh, sl = _split(s)
    wh, wl = _split(w_ref[...])
    o_ref[...] = _dot3(sh, sl, wh, wl) + b_ref[...]


def _modulation(cond, w_mod, b_mod):
    return pl.pallas_call(
        _mod_kernel,
        grid=(DEPTH, N_MOD // MOD_TN),
        in_specs=[
            pl.BlockSpec((COND_ROWS, D_MODEL), lambda l, j: (0, 0)),
            pl.BlockSpec((None, D_MODEL, MOD_TN), lambda l, j: (l, 0, j)),
            pl.BlockSpec((None, 1, MOD_TN), lambda l, j: (l, 0, j)),
        ],
        out_specs=pl.BlockSpec((None, COND_ROWS, MOD_TN), lambda l, j: (l, 0, j)),
        out_shape=jax.ShapeDtypeStruct((DEPTH, COND_ROWS, N_MOD), F32),
        compiler_params=_cparams(2),
        name="adaln_mod",
    )(cond, w_mod, b_mod.reshape(DEPTH, 1, N_MOD))


def _mod_spec(layer, first_row, rows_per_group):
    tiles_per_group = rows_per_group // ROW_TILE
    base = layer * COND_ROWS + first_row
    return pl.BlockSpec((None, 1, N_MOD), lambda i: (base + i // tiles_per_group, 0, 0))


def _in_kernel(x_ref, mod_ref, nw_ref, w_ref, p_ref):
    x = x_ref[...]
    sh1 = mod_ref[:, 0:D_MODEL]
    sc1 = mod_ref[:, D_MODEL:2 * D_MODEL]
    h = (x * _rms_scale(x) * nw_ref[...]) * (1.0 + sc1) + sh1
    p_ref[...] = _dot(h.astype(BF16), w_ref[...])


def _in_proj(x, mod, mod_spec, layer, norm_w, w_in_b):
    n = x.shape[0]
    return pl.pallas_call(
        _in_kernel,
        grid=(n // ROW_TILE,),
        in_specs=[
            pl.BlockSpec((ROW_TILE, D_MODEL), lambda i: (i, 0)),
            mod_spec,
            _layer_spec((1, D_MODEL), layer, 1),
            _layer_spec((D_MODEL, D_IN), layer, 1),
        ],
        out_specs=pl.BlockSpec((ROW_TILE, D_IN), lambda i: (i, 0)),
        out_shape=jax.ShapeDtypeStruct((n, D_IN), F32),
        compiler_params=_cparams(1),
        name="in_proj",
    )(x, mod, norm_w, w_in_b)


ATT_QB = 256
Q_GROUP_W = GQA_G * HEAD_DIM


def _head_sum_matrix(width):
    idx = np.arange(width) // HEAD_DIM
    return jnp.asarray((idx[:, None] == idx[None, :]).astype(np.float32), dtype=BF16)


def _kv_tile_matrix():
    src = np.arange(KV_W)
    dst = np.arange(Q_GROUP_W)
    m = np.stack([(src[:, None] == (j * HEAD_DIM + dst[None, :] % HEAD_DIM)) for j in range(N_KV)])
    return jnp.asarray(m.astype(np.float32), dtype=BF16)


def _head_rmsnorm(x, w, gsum):
    ms = _dot((x * x).astype(BF16), gsum) * (1.0 / HEAD_DIM)
    return x * jax.lax.rsqrt(ms + EPS) * w


def _rope(x, cos, sin):
    w = x.shape[-1]
    lane = jax.lax.broadcasted_iota(jnp.int32, x.shape, 1)
    up = pltpu.roll(x, w - HEAD_DIM // 4, axis=1)
    down = pltpu.roll(x, HEAD_DIM // 4, axis=1)
    partner = jnp.where((lane % (HEAD_DIM // 2)) < HEAD_DIM // 4, up, down)
    return x * cos + partner * sin


def _attend(qn, kexp_ref, vexp_ref, o_ref):
    rows = qn.shape[0]
    qb = (qn * (HEAD_DIM ** -0.5)).astype(BF16)
    lane = jax.lax.broadcasted_iota(jnp.int32, (rows, Q_GROUP_W), 1)
    for j in range(N_KV):
        qj = qb[:, j * Q_GROUP_W:(j + 1) * Q_GROUP_W]
        kx = kexp_ref[j]
        vx = vexp_ref[j]
        out = jnp.zeros((rows, Q_GROUP_W), F32)
        for g in range(GQA_G):
            mask = (lane // HEAD_DIM) == g
            qm = jnp.where(mask, qj, jnp.zeros_like(qj))
            s = _dot_nt(qm, kx)
            m = jnp.max(s, axis=-1, keepdims=True)
            e = jnp.exp(s - m)
            inv = 1.0 / jnp.sum(e, axis=-1, keepdims=True)
            og = _dot(e.astype(BF16), vx) * inv
            out = jnp.where(mask, og, out)
        o_ref[:, j * Q_GROUP_W:(j + 1) * Q_GROUP_W] = out


def _attn_ctx_kernel(q_ref, kv_ref, qw_ref, kw_ref, gq_ref, gk_ref, tile_ref, *rest):
    o_ref, kn_ref, v_ref, kexp, vexp = rest[-5:]
    kv = kv_ref[...]
    kn = _head_rmsnorm(kv[:, :KV_W], kw_ref[...], gk_ref[...])
    v = kv[:, KV_W:]
    kn_ref[...] = kn
    v_ref[...] = v
    kb = kn.astype(BF16)
    vb = v.astype(BF16)
    for j in range(N_KV):
        kexp[j] = _dot(kb, tile_ref[j]).astype(BF16)
        vexp[j] = _dot(vb, tile_ref[j]).astype(BF16)
    qn = _head_rmsnorm(q_ref[...], qw_ref[...], gq_ref[...])
    _attend(qn, kexp, vexp, o_ref)


def _attn_ctx(p, n_batch, seq, layer, qw, kw, new_k, new_v):
    n = p.shape[0]
    kv_shape = jax.ShapeDtypeStruct((n_batch, DEPTH, seq, KV_W), F32)
    in_specs = [
        pl.BlockSpec((seq, ATT_W), lambda b: (b, 0)),
        pl.BlockSpec((seq, 2 * KV_W), lambda b: (b, OFF_KV // (2 * KV_W))),
        _layer_spec((1, ATT_W), layer, 1),
        _layer_spec((1, KV_W), layer, 1),
        _const_spec((ATT_W, ATT_W), 1),
        _const_spec((KV_W, KV_W), 1),
        _const_spec((N_KV, KV_W, Q_GROUP_W), 1),
    ]
    args = [p, p, qw, kw, _head_sum_matrix(ATT_W), _head_sum_matrix(KV_W), _kv_tile_matrix()]
    aliases = {}
    if new_k is not None:
        in_specs += [pl.BlockSpec(memory_space=pl.ANY)] * 2
        aliases = {len(args): 1, len(args) + 1: 2}
        args += [new_k, new_v]
    return pl.pallas_call(
        _attn_ctx_kernel,
        grid=(n_batch,),
        in_specs=in_specs,
        out_specs=[
            pl.BlockSpec((seq, ATT_W), lambda b: (b, 0)),
            pl.BlockSpec((None, None, seq, KV_W), lambda b: (b, layer, 0, 0)),
            pl.BlockSpec((None, None, seq, KV_W), lambda b: (b, layer, 0, 0)),
        ],
        out_shape=[jax.ShapeDtypeStruct((n, ATT_W), F32), kv_shape, kv_shape],
        scratch_shapes=[pltpu.VMEM((N_KV, seq, Q_GROUP_W), BF16),
                        pltpu.VMEM((N_KV, seq, Q_GROUP_W), BF16)],
        input_output_aliases=aliases,
        compiler_params=_cparams(1),
        name="attn_ctx",
    )(*args)


def _attn_lat_kernel(q_ref, kv_ref, ck_ref, cv_ref, qw_ref, kw_ref, gq_ref, gk_ref, tile_ref,
                     cq_ref, sq_ref, ck_rope_ref, sk_rope_ref, o_ref, kexp, vexp, *, past, seq):
    @pl.when(pl.program_id(1) == 0)
    def _():
        kv = kv_ref[...]
        kn = _head_rmsnorm(kv[:, :KV_W], kw_ref[...], gk_ref[...])
        kn = _rope(kn, ck_rope_ref[...], sk_rope_ref[...])
        kb = kn.astype(BF16)
        vb = kv[:, KV_W:].astype(BF16)
        ckb = ck_ref[...].astype(BF16)
        cvb = cv_ref[...].astype(BF16)
        for j in range(N_KV):
            kexp[j, 0:past, :] = _dot(ckb, tile_ref[j]).astype(BF16)
            vexp[j, 0:past, :] = _dot(cvb, tile_ref[j]).astype(BF16)
            kexp[j, past:past + seq, :] = _dot(kb, tile_ref[j]).astype(BF16)
            vexp[j, past:past + seq, :] = _dot(vb, tile_ref[j]).astype(BF16)

    qn = _head_rmsnorm(q_ref[...], qw_ref[...], gq_ref[...])
    qn = _rope(qn, cq_ref[...], sq_ref[...])
    _attend(qn, kexp, vexp, o_ref)


def _rope_tables(seq, width):
    rows = seq // GRID_W
    row = jnp.repeat(jnp.arange(rows, dtype=F32), GRID_W)
    col = jnp.tile(jnp.arange(GRID_W, dtype=F32), rows)
    nq = HEAD_DIM // 4
    freqs = ROPE_BASE ** (-jnp.arange(nq, dtype=F32) / nq)
    ar = row[:, None] * freqs[None, :]
    ac = col[:, None] * freqs[None, :]
    cos = jnp.concatenate([jnp.cos(ar), jnp.cos(ar), jnp.cos(ac), jnp.cos(ac)], axis=-1)
    sin = jnp.concatenate([-jnp.sin(ar), jnp.sin(ar), -jnp.sin(ac), jnp.sin(ac)], axis=-1)
    reps = width // HEAD_DIM
    return jnp.tile(cos, (1, reps)), jnp.tile(sin, (1, reps))


def _attn_lat(p, cache_k, cache_v, layer, n_batch, seq, past, qw, kw, tables):
    n = p.shape[0]
    nqb = seq // ATT_QB
    cq, sq, ck, sk = tables
    lk = past + seq
    return pl.pallas_call(
        functools.partial(_attn_lat_kernel, past=past, seq=seq),
        grid=(n_batch, nqb),
        in_specs=[
            pl.BlockSpec((ATT_QB, ATT_W), lambda b, i: (b * nqb + i, 0)),
            pl.BlockSpec((seq, 2 * KV_W), lambda b, i: (b, OFF_KV // (2 * KV_W))),
            pl.BlockSpec((None, None, past, KV_W), lambda b, i: (b, layer, 0, 0)),
            pl.BlockSpec((None, None, past, KV_W), lambda b, i: (b, layer, 0, 0)),
            _layer_spec((1, ATT_W), layer, 2),
            _layer_spec((1, KV_W), layer, 2),
            _const_spec((ATT_W, ATT_W), 2),
            _const_spec((KV_W, KV_W), 2),
            _const_spec((N_KV, KV_W, Q_GROUP_W), 2),
            pl.BlockSpec((ATT_QB, ATT_W), lambda b, i: (i, 0)),
            pl.BlockSpec((ATT_QB, ATT_W), lambda b, i: (i, 0)),
            _const_spec((seq, KV_W), 2),
            _const_spec((seq, KV_W), 2),
        ],
        out_specs=pl.BlockSpec((ATT_QB, ATT_W), lambda b, i: (b * nqb + i, 0)),
        out_shape=jax.ShapeDtypeStruct((n, ATT_W), F32),
        scratch_shapes=[pltpu.VMEM((N_KV, lk, Q_GROUP_W), BF16),
                        pltpu.VMEM((N_KV, lk, Q_GROUP_W), BF16)],
        compiler_params=_cparams(2),
        name="attn_lat",
    )(p, p, cache_k, cache_v, qw, kw, _head_sum_matrix(ATT_W), _head_sum_matrix(KV_W),
      _kv_tile_matrix(), cq, sq, ck, sk)


def _shift_rows(x, d, fill):
    n = x.shape[0]
    t = jax.lax.broadcasted_iota(jnp.int32, x.shape, 0)
    r = pltpu.roll(x, d % n, axis=0)
    valid = (t >= d) if d > 0 else (t < n + d)
    return jnp.where(valid, r, jnp.full_like(r, fill))


def _tile_scan(a, b, reverse):
    n, w = a.shape
    a3 = a.reshape(n // SUBLANES, SUBLANES, w)
    b3 = b.reshape(n // SUBLANES, SUBLANES, w)
    r = jax.lax.broadcasted_iota(jnp.int32, a3.shape, 1)
    d = 1
    while d < SUBLANES:
        valid = (r < SUBLANES - d) if reverse else (r >= d)
        shift = SUBLANES - d if reverse else d
        b3 = a3 * jnp.where(valid, pltpu.roll(b3, shift, axis=1), 0.0) + b3
        a3 = a3 * jnp.where(valid, pltpu.roll(a3, shift, axis=1), 1.0)
        d *= 2
    return a3.reshape(n, w), b3.reshape(n, w)


def _gelu_tanh(x):
    return 0.5 * x * (1.0 + jnp.tanh(math.sqrt(2.0 / math.pi) * (x + 0.044715 * (x * x * x))))


LRU_CARRY_UNROLL = 32


def _lru_kernel(lx_ref, lg_ref, cw_ref, cb_ref, gw_ref, gb_ref, lam_ref, h0_ref, o_ref, st_ref,
                a_scr, b_scr, h_scr):
    lx = lx_ref[...]
    cw = cw_ref[...]
    xc = (cw[0:1] * _shift_rows(lx, 2, 0.0) + cw[1:2] * _shift_rows(lx, 1, 0.0)
          + cw[2:3] * lx + cw[3:4] * _shift_rows(lx, -1, 0.0)) + cb_ref[...]
    gates = _dot(xc.astype(BF16), gw_ref[...]) + gb_ref[...]
    neg = -lam_ref[...]
    softplus = jnp.maximum(neg, 0.0) + jnp.log1p(jnp.exp(-jnp.abs(neg)))
    for dirn in range(2):
        r = jax.nn.sigmoid(gates[:, (2 * dirn) * LRU_W:(2 * dirn + 1) * LRU_W])
        i = jax.nn.sigmoid(gates[:, (2 * dirn + 1) * LRU_W:(2 * dirn + 2) * LRU_W])
        log_a = -LRU_C * r * softplus[dirn:dirn + 1]
        a = jnp.exp(log_a)
        mult = jnp.sqrt(-jnp.tanh(log_a) * (a * a + 1.0))
        b = mult * (i * xc)
        a_scr[dirn], b_scr[dirn] = _tile_scan(a, b, reverse=(dirn == 1))

    nt = lx.shape[0] // SUBLANES

    def carry_step(k, carry):
        cf, cb = carry
        rf = pl.ds(pl.multiple_of(k * SUBLANES, SUBLANES), SUBLANES)
        rb = pl.ds(pl.multiple_of((nt - 1 - k) * SUBLANES, SUBLANES), SUBLANES)
        hf = a_scr[0, rf, :] * cf + b_scr[0, rf, :]
        hb = a_scr[1, rb, :] * cb + b_scr[1, rb, :]
        h_scr[0, rf, :] = hf
        h_scr[1, rb, :] = hb
        return hf[SUBLANES - 1:SUBLANES, :], hb[0:1, :]

    h0 = h0_ref[...]
    cf, cb = jax.lax.fori_loop(0, nt, carry_step, (h0[0:1], h0[1:2]), unroll=LRU_CARRY_UNROLL)
    st_ref[0:1, :] = cf
    st_ref[1:2, :] = cb
    o_ref[...] = _gelu_tanh(lg_ref[...]) * (h_scr[0] + h_scr[1])


def _lru(p, n_batch, seq, layer, cw, cb, gw, gb, lam, h0, h0_spec):
    n = p.shape[0]
    return pl.pallas_call(
        _lru_kernel,
        grid=(n_batch,),
        in_specs=[
            pl.BlockSpec((seq, LRU_W), lambda b: (b, OFF_LX // LRU_W)),
            pl.BlockSpec((seq, LRU_W), lambda b: (b, OFF_LG // LRU_W)),
            _layer_spec((4, LRU_W), layer, 1),
            _layer_spec((1, LRU_W), layer, 1),
            _layer_spec((LRU_W, 4 * LRU_W), layer, 1),
            _layer_spec((1, 4 * LRU_W), layer, 1),
            _layer_spec((2, LRU_W), layer, 1),
            h0_spec,
        ],
        out_specs=[
            pl.BlockSpec((seq, LRU_W), lambda b: (b, 0)),
            pl.BlockSpec((None, 2, LRU_W), lambda b: (b, 0, 0)),
        ],
        out_shape=[
            jax.ShapeDtypeStruct((n, LRU_W), F32),
            jax.ShapeDtypeStruct((n_batch, 2, LRU_W), F32),
        ],
        scratch_shapes=[pltpu.VMEM((2, seq, LRU_W), F32)] * 3,
        compiler_params=_cparams(1),
        name="rglru",
    )(p, p, cw, cb, gw, gb, lam, h0)


def _gate_matrix(gate_w):
    eye = jnp.eye(LRU_BLOCKS, dtype=gate_w.dtype)
    dense = jnp.einsum('dgnij,nm->dgnimj', gate_w, eye).reshape(2, 2, LRU_W, LRU_W)
    return dense.transpose(2, 0, 1, 3).reshape(LRU_W, 4 * LRU_W)


def _dft_matrices(seq):
    n2 = 2 * seq
    f = jnp.arange(seq, dtype=jnp.int32)
    t = jnp.arange(seq, dtype=jnp.int32)
    ang = ((f[:, None] * t[None, :]) % n2).astype(F32) * (2.0 * math.pi / n2)
    cos = jnp.cos(ang)
    sin = jnp.sin(ang)
    alt = jnp.where(t % 2 == 0, 1.0, -1.0).astype(F32)
    f0 = (f == 0)[:, None]
    fwd_re = cos
    fwd_im = jnp.where(f0, alt[None, :], -sin)
    wgt = jnp.where(f0, 1.0, 2.0) / n2
    inv_re = (cos * wgt).T
    inv_im = jnp.where(f0, alt[None, :] / n2, -sin * wgt).T
    nchunk = seq // FREQ_CHUNK
    fwd = jnp.concatenate([fwd_re.reshape(nchunk, FREQ_CHUNK, seq), fwd_im.reshape(nchunk, FREQ_CHUNK, seq)], axis=1)
    inv = jnp.concatenate([inv_re.reshape(seq, nchunk, FREQ_CHUNK), inv_im.reshape(seq, nchunk, FREQ_CHUNK)], axis=2)
    inv = inv.transpose(1, 0, 2)
    return fwd.astype(BF16), inv.astype(BF16)


def _filter_features(seq):
    t = jnp.arange(seq, dtype=F32)
    tn = t / seq
    bands = jnp.linspace(1e-4, HY_BANDS - 1, HY_BANDS, dtype=F32)
    ang = (2.0 * math.pi / seq) * t[:, None] * bands[None, :]
    z = jnp.concatenate([tn[:, None], jnp.cos(ang), -jnp.sin(ang)], axis=-1)
    return jnp.pad(z, ((0, 0), (0, FILT_PAD - HY_FEAT)))


def _filter_kernel(z_ref, w1_ref, b1_ref, w2_ref, b2_ref, w3_ref, b3_ref, dl_ref, f_ref,
                   kr_ref, ki_ref, taps):
    c = pl.program_id(1)

    @pl.when(c == 0)
    def _():
        z = z_ref[...]
        zh, zl = _split(z)
        w1h, w1l = _split(w1_ref[...])
        h = jnp.sin(_dot3(zh, zl, w1h, w1l) + b1_ref[...])
        hh, hl = _split(h)
        w2h, w2l = _split(w2_ref[...])
        h = jnp.sin(_dot3(hh, hl, w2h, w2l) + b2_ref[...])
        hh, hl = _split(h)
        w3h, w3l = _split(w3_ref[...])
        hf = _dot3(hh, hl, w3h, w3l) + b3_ref[...]
        decay = jnp.exp(-z[:, 0:1] * jnp.abs(dl_ref[...]))
        row = jax.lax.broadcasted_iota(jnp.int32, decay.shape, 0)
        for o in range(2):
            for s in range(2):
                cols = slice((2 * o + s) * HY_W, (2 * o + s + 1) * HY_W)
                blk = hf[:, cols] * decay
                if s == 1:
                    blk = jnp.where(row == 0, 0.0, blk)
                taps[:, cols] = blk.astype(BF16)

    spec = _dot(f_ref[...], taps[...])
    re = spec[:FREQ_CHUNK]
    im = spec[FREQ_CHUNK:]
    row = jax.lax.broadcasted_iota(jnp.int32, (FREQ_CHUNK, HY_W), 0)
    nyq = jnp.logical_and(row == 0, c == 0)
    for o in range(2):
        a = slice((2 * o) * HY_W, (2 * o + 1) * HY_W)
        b = slice((2 * o + 1) * HY_W, (2 * o + 2) * HY_W)
        kr_ref[o] = re[:, a] + re[:, b]
        ki_ref[o] = jnp.where(nyq, im[:, a] + im[:, b], im[:, a] - im[:, b])


def _filter_spectra(seq, w1p, b1p, w2p, b2p, w3p, b3, fwd):
    nchunk = seq // FREQ_CHUNK
    deltas = jnp.linspace(math.log(HY_DECAY_TARGET) / HY_DECAY_SLOW,
                          math.log(HY_DECAY_TARGET) / HY_DECAY_FAST, HY_W, dtype=F32).reshape(1, HY_W)
    out = jax.ShapeDtypeStruct((DEPTH, 2, nchunk, FREQ_CHUNK, HY_W), F32)
    return pl.pallas_call(
        _filter_kernel,
        grid=(DEPTH, nchunk),
        in_specs=[
            pl.BlockSpec((seq, FILT_PAD), lambda l, c: (0, 0)),
            pl.BlockSpec((None, FILT_PAD, FILT_PAD), lambda l, c: (l, 0, 0)),
            pl.BlockSpec((None, 1, FILT_PAD), lambda l, c: (l, 0, 0)),
            pl.BlockSpec((None, FILT_PAD, FILT_PAD), lambda l, c: (l, 0, 0)),
            pl.BlockSpec((None, 1, FILT_PAD), lambda l, c: (l, 0, 0)),
            pl.BlockSpec((None, FILT_PAD, 4 * HY_W), lambda l, c: (l, 0, 0)),
            pl.BlockSpec((None, 1, 4 * HY_W), lambda l, c: (l, 0, 0)),
            pl.BlockSpec((1, HY_W), lambda l, c: (0, 0)),
            pl.BlockSpec((None, 2 * FREQ_CHUNK, seq), lambda l, c: (c, 0, 0)),
        ],
        out_specs=[
            pl.BlockSpec((None, 2, None, FREQ_CHUNK, HY_W), lambda l, c: (l, 0, c, 0, 0)),
            pl.BlockSpec((None, 2, None, FREQ_CHUNK, HY_W), lambda l, c: (l, 0, c, 0, 0)),
        ],
        out_shape=[out, out],
        scratch_shapes=[pltpu.VMEM((seq, 4 * HY_W), BF16)],
        compiler_params=_cparams(2),
        name="hyena_filter",
    )(_filter_features(seq), w1p, b1p, w2p, b2p, w3p, b3, deltas, fwd)


HY_ROWS = 2048


def _hyena_kernel(v_ref, x1_ref, x2_ref, cw_ref, skip_ref, kr_ref, ki_ref, f_ref, i_ref,
                  o_ref, u_b, u_f32, x1c, x2c, y_b, acc, *, nchunk, nb, seq):
    o = pl.program_id(1)
    c = pl.program_id(2)
    col_blocks = [slice(i * HY_W, (i + 1) * HY_W) for i in range(nb)]

    def conv3(x_ref, k, i):
        x = x_ref[i * seq:(i + 1) * seq, :]
        w = cw_ref[:, k * HY_W:(k + 1) * HY_W]
        return w[0:1] * _shift_rows(x, 1, 0.0) + w[1:2] * x + w[2:3] * _shift_rows(x, -1, 0.0)

    def set_u(cols, u):
        u_f32[:, cols] = u
        u_b[:, cols] = u.astype(BF16)

    @pl.when(jnp.logical_and(o == 0, c == 0))
    def _():
        for i, cols in enumerate(col_blocks):
            set_u(cols, conv3(v_ref, 0, i))
            x1c[:, cols] = conv3(x1_ref, 1, i)
            x2c[:, cols] = conv3(x2_ref, 2, i)

    spec = _dot(f_ref[...], u_b[...])
    kr = kr_ref[...]
    ki = ki_ref[...]
    row = jax.lax.broadcasted_iota(jnp.int32, (FREQ_CHUNK, HY_W), 0)
    packed = jnp.logical_and(row == 0, c == 0)
    ki_x = jnp.where(packed, 0.0, ki)
    kr_i = jnp.where(packed, ki, kr)
    for cols in col_blocks:
        ur = spec[:FREQ_CHUNK, cols]
        ui = spec[FREQ_CHUNK:, cols]
        y_b[:FREQ_CHUNK, cols] = (ur * kr - ui * ki_x).astype(BF16)
        y_b[FREQ_CHUNK:, cols] = (ur * ki_x + ui * kr_i).astype(BF16)
    part = _dot(i_ref[...], y_b[...])

    def finish(total):
        @pl.when(o == 0)
        def _():
            for cols in col_blocks:
                set_u(cols, x1c[:, cols] * (total[:, cols] + u_f32[:, cols] * skip_ref[0:1, :]))

        @pl.when(o == 1)
        def _():
            for i, cols in enumerate(col_blocks):
                o_ref[i * seq:(i + 1) * seq, :] = x2c[:, cols] * (total[:, cols] + u_f32[:, cols] * skip_ref[1:2, :])

    if nchunk == 1:
        finish(part)
    else:
        @pl.when(c == 0)
        def _():
            acc[...] = part

        @pl.when(c != 0)
        def _():
            acc[...] += part

        @pl.when(c == nchunk - 1)
        def _():
            finish(acc[...])


def _hyena(p, n_batch, seq, layer, cw, skip, kr, ki, dft):
    n = p.shape[0]
    nchunk = seq // FREQ_CHUNK
    nb = HY_ROWS // seq
    fwd, inv = dft
    blk0 = OFF_HY // HY_W
    wide = (seq, nb * HY_W)
    vm = pltpu.VMEM
    return pl.pallas_call(
        functools.partial(_hyena_kernel, nchunk=nchunk, nb=nb, seq=seq),
        grid=(n_batch // nb, 2, nchunk),
        in_specs=[
            pl.BlockSpec((HY_ROWS, HY_W), lambda b, o, c: (b, blk0)),
            pl.BlockSpec((HY_ROWS, HY_W), lambda b, o, c: (b, blk0 + 1)),
            pl.BlockSpec((HY_ROWS, HY_W), lambda b, o, c: (b, blk0 + 2)),
            _layer_spec((3, 3 * HY_W), layer, 3),
            _layer_spec((2, HY_W), layer, 3),
            pl.BlockSpec((None, None, None, FREQ_CHUNK, HY_W), lambda b, o, c: (layer, o, c, 0, 0)),
            pl.BlockSpec((None, None, None, FREQ_CHUNK, HY_W), lambda b, o, c: (layer, o, c, 0, 0)),
            pl.BlockSpec((None, 2 * FREQ_CHUNK, seq), lambda b, o, c: (c, 0, 0)),
            pl.BlockSpec((None, seq, 2 * FREQ_CHUNK), lambda b, o, c: (c, 0, 0)),
        ],
        out_specs=pl.BlockSpec((HY_ROWS, HY_W), lambda b, o, c: (b, 0)),
        out_shape=jax.ShapeDtypeStruct((n, HY_W), F32),
        scratch_shapes=[vm(wide, BF16), vm(wide, F32), vm(wide, F32), vm(wide, F32),
                        vm((2 * FREQ_CHUNK, nb * HY_W), BF16),
                        vm(wide if nchunk > 1 else (SUBLANES, LANES), F32)],
        compiler_params=_cparams(3),
        name="hyena",
    )(p, p, p, cw, skip, kr, ki, fwd, inv)


FF_CHUNK = D_FF // 2


def _out_ffn_kernel(x_ref, att_ref, lru_ref, hy_ref, mod_ref, nw_ref, wo_ref, w1_ref, w3_ref, w2_ref,
                    o_ref, t_scr):
    g1 = mod_ref[:, 2 * D_MODEL:3 * D_MODEL]
    sh2 = mod_ref[:, 3 * D_MODEL:4 * D_MODEL]
    sc2 = mod_ref[:, 4 * D_MODEL:5 * D_MODEL]
    g2 = mod_ref[:, 5 * D_MODEL:6 * D_MODEL]
    mix = jnp.concatenate([att_ref[...], lru_ref[...], hy_ref[...]], axis=1).astype(BF16)
    x1 = x_ref[...] + g1 * _dot(mix, wo_ref[...])
    h2 = ((x1 * _rms_scale(x1) * nw_ref[...]) * (1.0 + sc2) + sh2).astype(BF16)
    for k in range(D_FF // FF_CHUNK):
        cols = slice(k * FF_CHUNK, (k + 1) * FF_CHUNK)
        a = _dot(h2, w1_ref[:, cols])
        b = _dot(h2, w3_ref[:, cols])
        t_scr[:, cols] = ((a * jax.nn.sigmoid(a)) * b).astype(BF16)
    o_ref[...] = x1 + g2 * _dot(t_scr[...], w2_ref[...])


def _resident(shape, layer):
    zeros = (0,) * len(shape)
    return pl.BlockSpec((None,) + shape, lambda i: (layer,) + zeros, pipeline_mode=pl.Buffered(1))


def _out_ffn(x, att, lru, hy, mod, mod_spec, layer, norm_w, wo_b, w1_b, w3_b, w2_b):
    n = x.shape[0]
    return pl.pallas_call(
        _out_ffn_kernel,
        grid=(n // ROW_TILE,),
        in_specs=[
            pl.BlockSpec((ROW_TILE, D_MODEL), lambda i: (i, 0)),
            pl.BlockSpec((ROW_TILE, ATT_W), lambda i: (i, 0)),
            pl.BlockSpec((ROW_TILE, LRU_W), lambda i: (i, 0)),
            pl.BlockSpec((ROW_TILE, HY_W), lambda i: (i, 0)),
            mod_spec,
            _layer_spec((1, D_MODEL), layer, 1),
            _resident((D_MODEL, D_MODEL), layer),
            _resident((D_MODEL, D_FF), layer),
            _resident((D_MODEL, D_FF), layer),
            _resident((D_FF, D_MODEL), layer),
        ],
        out_specs=pl.BlockSpec((ROW_TILE, D_MODEL), lambda i: (i, 0)),
        out_shape=jax.ShapeDtypeStruct((n, D_MODEL), F32),
        scratch_shapes=[pltpu.VMEM((ROW_TILE, D_FF), BF16)],
        compiler_params=_cparams(1),
        name="out_ffn",
    )(x, att, lru, hy, mod, norm_w, wo_b, w1_b, w3_b, w2_b)


def kernel(x_prompt, x_sample, cache_k, cache_v, state_lru, c, c_ctx, w_mod, b_mod, norm1_w, norm2_w, w_in, q_norm_w, k_norm_w, lru_conv_w, lru_conv_b, lru_gate_w, lru_gate_b, lru_lambda, hy_conv_w, hy_filt_w1, hy_filt_b1, hy_filt_w2, hy_filt_b2, hy_filt_w3, hy_filt_b3, hy_skip, w_out, ffn_w1, ffn_w3, ffn_w2):
    batch, seq, _ = x_prompt.shape
    dec_batch, dec_seq, _ = x_sample.shape
    past = cache_k.shape[2]
    assert 1 + dec_batch <= COND_ROWS

    cond = jnp.concatenate([c_ctx[None, :], c, jnp.zeros((COND_ROWS - 1 - dec_batch, D_MODEL), F32)], axis=0)
    mod = _modulation(cond, w_mod, b_mod).reshape(DEPTH * COND_ROWS, 1, N_MOD)

    w_in_b = w_in.astype(BF16)
    w_out_b = w_out.astype(BF16)
    w1_b = ffn_w1.astype(BF16)
    w3_b = ffn_w3.astype(BF16)
    w2_b = ffn_w2.astype(BF16)
    norm1 = norm1_w.reshape(DEPTH, 1, D_MODEL)
    norm2 = norm2_w.reshape(DEPTH, 1, D_MODEL)
    qw = jnp.tile(q_norm_w, (1, N_HEADS)).reshape(DEPTH, 1, ATT_W)
    kw = jnp.tile(k_norm_w, (1, N_KV)).reshape(DEPTH, 1, KV_W)
    lru_w = (lru_conv_w, lru_conv_b.reshape(DEPTH, 1, LRU_W),
             jax.vmap(_gate_matrix)(lru_gate_w).astype(BF16), lru_gate_b.reshape(DEPTH, 1, 4 * LRU_W), lru_lambda)
    pad_f = FILT_PAD - HY_FEAT
    pad_h = FILT_PAD - HY_FILT_HID
    w1p = jnp.pad(hy_filt_w1, ((0, 0), (0, pad_f), (0, pad_h)))
    b1p = jnp.pad(hy_filt_b1, ((0, 0), (0, pad_h))).reshape(DEPTH, 1, FILT_PAD)
    w2p = jnp.pad(hy_filt_w2, ((0, 0), (0, pad_h), (0, pad_h)))
    b2p = jnp.pad(hy_filt_b2, ((0, 0), (0, pad_h))).reshape(DEPTH, 1, FILT_PAD)
    w3p = jnp.pad(hy_filt_w3, ((0, 0), (0, pad_h), (0, 0)))
    b3 = hy_filt_b3.reshape(DEPTH, 1, 4 * HY_W)

    dft_ctx = _dft_matrices(seq)
    dft_lat = _dft_matrices(dec_seq)
    kr_ctx, ki_ctx = _filter_spectra(seq, w1p, b1p, w2p, b2p, w3p, b3, dft_ctx[0])
    kr_lat, ki_lat = _filter_spectra(dec_seq, w1p, b1p, w2p, b2p, w3p, b3, dft_lat[0])
    rope = _rope_tables(dec_seq, ATT_W) + _rope_tables(dec_seq, KV_W)

    ck = cache_k.reshape(dec_batch, DEPTH, past, KV_W)
    cv = cache_v.reshape(dec_batch, DEPTH, past, KV_W)
    zero_state = jnp.zeros((1, 2, LRU_W), F32)
    zero_state_spec = pl.BlockSpec((None, 2, LRU_W), lambda b: (0, 0, 0))

    y_p = x_prompt.reshape(batch * seq, D_MODEL)
    y_s = x_sample.reshape(dec_batch * dec_seq, D_MODEL)
    new_k = new_v = None
    states = []
    for l in range(DEPTH):
        mod_ctx = _mod_spec(l, 0, batch * seq)
        mod_lat = _mod_spec(l, 1, dec_seq)
        lat_state_spec = pl.BlockSpec((None, None, 2, LRU_W), lambda b, l=l: (b, l, 0, 0))

        p = _in_proj(y_p, mod, mod_ctx, l, norm1, w_in_b)
        att, new_k, new_v = _attn_ctx(p, batch, seq, l, qw, kw, new_k, new_v)
        lru, st = _lru(p, batch, seq, l, *lru_w, zero_state, zero_state_spec)
        hy = _hyena(p, batch, seq, l, hy_conv_w, hy_skip, kr_ctx, ki_ctx, dft_ctx)
        y_p = _out_ffn(y_p, att, lru, hy, mod, mod_ctx, l, norm2, w_out_b, w1_b, w3_b, w2_b)
        states.append(st)

        p = _in_proj(y_s, mod, mod_lat, l, norm1, w_in_b)
        att = _attn_lat(p, ck, cv, l, dec_batch, dec_seq, past, qw, kw, rope)
        lru, _ = _lru(p, dec_batch, dec_seq, l, *lru_w, state_lru, lat_state_spec)
        hy = _hyena(p, dec_batch, dec_seq, l, hy_conv_w, hy_skip, kr_lat, ki_lat, dft_lat)
        y_s = _out_ffn(y_s, att, lru, hy, mod, mod_lat, l, norm2, w_out_b, w1_b, w3_b, w2_b)

    return (y_p.reshape(batch, seq, D_MODEL), y_s.reshape(dec_batch, dec_seq, D_MODEL),
            new_k.reshape(batch, DEPTH, seq, N_KV, HEAD_DIM), new_v.reshape(batch, DEPTH, seq, N_KV, HEAD_DIM),
            jnp.stack(states, axis=1))
```
